```python
import jax, jax.numpy as jnp
from jax import lax
import numpy as np

D_MODEL = 2048
BATCH = 1
SEQ = 16384
DEPTH = 1
DEC_BATCH = 2
DEC_SEQ = 8192
PAST_LEN = 128

D_RNN = 2048
LRU_HEADS = 16
LRU_BLOCK = D_RNN // LRU_HEADS
LRU_C = 8.0
CONV_A = 4
CONV_A_PAD = (2, 1)
FOURIER_GROUPS = 4
D_FOURIER = 1024
FOURIER_GW = D_FOURIER // FOURIER_GROUPS
D_FF = 6144
CONV_F = 3
CONV_F_PAD = (1, 1)
EPS = 1e-6
D_IN = 2 * D_RNN + D_FOURIER + 2 * D_MODEL

kernel_name = "hawk_fnet_parallel_encoder"


def rmsnorm(x, g):
    xf = x.astype(jnp.float32)
    y = xf * lax.rsqrt(jnp.mean(xf * xf, axis=-1, keepdims=True) + EPS)
    return y * g.astype(jnp.float32)


def dwconv(x, w, b, pad):
    c = x.shape[-1]
    y = lax.conv_general_dilated(
        x, w[:, None, :].astype(x.dtype), window_strides=(1,), padding=[pad],
        dimension_numbers=("NWC", "WIO", "NWC"), feature_group_count=c)
    return y + b.astype(x.dtype)


def _lin_combine(e1, e2):
    a1, b1 = e1
    a2, b2 = e2
    return a1 * a2, a2 * b1 + b2


def linear_scan(a, bx, reverse):
    return lax.associative_scan(_lin_combine, (a, bx), axis=1, reverse=reverse)[1]


def rglru_bidir(u, w_a, b_a, w_x, b_x, lam):
    bsz, s, _ = u.shape
    ub = u.reshape(bsz, s, LRU_HEADS, LRU_BLOCK)
    r = jax.nn.sigmoid(jnp.einsum("bshi,dhij->dbshj", ub, w_a.astype(jnp.float32)).reshape(2, bsz, s, D_RNN)
                       + b_a.astype(jnp.float32)[:, None, None, :])
    i = jax.nn.sigmoid(jnp.einsum("bshi,dhij->dbshj", ub, w_x.astype(jnp.float32)).reshape(2, bsz, s, D_RNN)
                       + b_x.astype(jnp.float32)[:, None, None, :])
    log_a = LRU_C * r * jax.nn.log_sigmoid(lam.astype(jnp.float32))[:, None, None, :]
    a = jnp.exp(log_a)
    mult = jnp.sqrt(jnp.maximum(-jnp.expm1(2.0 * log_a), 0.0))
    bx = mult * i * u[None]
    h_fwd = linear_scan(a[0], bx[0], reverse=False)
    h_bwd = linear_scan(a[1], bx[1], reverse=True)
    return h_fwd + h_bwd


def fourier_mix(v):
    bsz, s, _ = v.shape
    vg = v.astype(jnp.float32).reshape(bsz, s, FOURIER_GROUPS, FOURIER_GW)
    f = jnp.fft.fftn(vg, axes=(1, 3), norm="ortho").real
    return f.reshape(bsz, s, D_FOURIER).astype(jnp.float32)


def trunk(x, g_mix, w_in, b_in, conv_a_w, conv_a_b, lru_w_a, lru_b_a, lru_w_x, lru_b_x, lru_lam,
          w_out_a, w_out_b, w_out, b_out, g_ffn, w_up, b_up, conv_f_w, conv_f_b, w_down, b_down, g_final):
    out_dtype = x.dtype
    h = x.astype(jnp.float32)
    for l in range(DEPTH):
        n = rmsnorm(h, g_mix[l])
        p = n @ w_in[l].astype(jnp.float32) + b_in[l].astype(jnp.float32)
        u_x, u_g, u_f, z_a, z_b = jnp.split(
            p, [D_RNN, 2 * D_RNN, 2 * D_RNN + D_FOURIER, 2 * D_RNN + D_FOURIER + D_MODEL], axis=-1)
        u_x = dwconv(u_x, conv_a_w[l], conv_a_b[l], CONV_A_PAD)
        hr = rglru_bidir(u_x, lru_w_a[l], lru_b_a[l], lru_w_x[l], lru_b_x[l], lru_lam[l])
        y_a = (jax.nn.gelu(u_g) * hr) @ w_out_a[l].astype(jnp.float32)
        y_b = fourier_mix(u_f) @ w_out_b[l].astype(jnp.float32)
        m = jax.nn.sigmoid(z_a) * y_a + jax.nn.sigmoid(z_b) * y_b
        h = h + m @ w_out[l].astype(jnp.float32) + b_out[l].astype(jnp.float32)
        n2 = rmsnorm(h, g_ffn[l])
        up = n2 @ w_up[l].astype(jnp.float32) + b_up[l].astype(jnp.float32)
        up = dwconv(up, conv_f_w[l], conv_f_b[l], CONV_F_PAD)
        gate, val = jnp.split(up, 2, axis=-1)
        h = h + (jax.nn.gelu(gate) * val) @ w_down[l].astype(jnp.float32) + b_down[l].astype(jnp.float32)
    return rmsnorm(h, g_final).astype(out_dtype)


def setup_inputs(seed: int = 0) -> dict:
    key = jax.random.key(seed)
    ks = jax.random.split(key, 32)
    f32 = jnp.float32
    nrm = lambda k, shape, scale: jax.random.normal(k, shape, f32) * scale
    x_prompt = jax.random.normal(ks[0], (BATCH, SEQ, D_MODEL), f32)
    x_sample = jax.random.normal(ks[1], (DEC_BATCH, DEC_SEQ, D_MODEL), f32)
    g_mix = 1.0 + nrm(ks[2], (DEPTH, D_MODEL), 0.02)
    w_in = nrm(ks[3], (DEPTH, D_MODEL, D_IN), D_MODEL ** -0.5)
    b_in = nrm(ks[4], (DEPTH, D_IN), 0.02)
    conv_a_w = nrm(ks[5], (DEPTH, CONV_A, D_RNN), CONV_A ** -0.5)
    conv_a_b = nrm(ks[6], (DEPTH, D_RNN), 0.02)
    lru_w_a = nrm(ks[7], (DEPTH, 2, LRU_HEADS, LRU_BLOCK, LRU_BLOCK), LRU_BLOCK ** -0.5)
    lru_b_a = nrm(ks[8], (DEPTH, 2, D_RNN), 0.02)
    lru_w_x = nrm(ks[9], (DEPTH, 2, LRU_HEADS, LRU_BLOCK, LRU_BLOCK), LRU_BLOCK ** -0.5)
    lru_b_x = nrm(ks[10], (DEPTH, 2, D_RNN), 0.02)
    a0 = jax.random.uniform(ks[11], (DEPTH, 2, D_RNN), f32, 0.9, 0.999)
    s = a0 ** (1.0 / LRU_C)
    lru_lam = jnp.log(s) - jnp.log1p(-s)
    w_out_a = nrm(ks[12], (DEPTH, D_RNN, D_MODEL), D_RNN ** -0.5)
    w_out_b = nrm(ks[13], (DEPTH, D_FOURIER, D_MODEL), D_FOURIER ** -0.5)
    w_out = nrm(ks[14], (DEPTH, D_MODEL, D_MODEL), D_MODEL ** -0.5)
    b_out = nrm(ks[15], (DEPTH, D_MODEL), 0.02)
    g_ffn = 1.0 + nrm(ks[16], (DEPTH, D_MODEL), 0.02)
    w_up = nrm(ks[17], (DEPTH, D_MODEL, 2 * D_FF), D_MODEL ** -0.5)
    b_up = nrm(ks[18], (DEPTH, 2 * D_FF), 0.02)
    conv_f_w = nrm(ks[19], (DEPTH, CONV_F, 2 * D_FF), CONV_F ** -0.5)
    conv_f_b = nrm(ks[20], (DEPTH, 2 * D_FF), 0.02)
    w_down = nrm(ks[21], (DEPTH, D_FF, D_MODEL), D_FF ** -0.5)
    b_down = nrm(ks[22], (DEPTH, D_MODEL), 0.02)
    g_final = 1.0 + nrm(ks[23], (D_MODEL,), 0.02)
    return {"x_prompt": x_prompt, "x_sample": x_sample, "g_mix": g_mix, "w_in": w_in, "b_in": b_in,
            "conv_a_w": conv_a_w, "conv_a_b": conv_a_b, "lru_w_a": lru_w_a, "lru_b_a": lru_b_a,
            "lru_w_x": lru_w_x, "lru_b_x": lru_b_x, "lru_lam": lru_lam, "w_out_a": w_out_a,
            "w_out_b": w_out_b, "w_out": w_out, "b_out": b_out, "g_ffn": g_ffn, "w_up": w_up,
            "b_up": b_up, "conv_f_w": conv_f_w, "conv_f_b": conv_f_b, "w_down": w_down,
            "b_down": b_down, "g_final": g_final}


def reference(x_prompt, x_sample, g_mix, w_in, b_in, conv_a_w, conv_a_b, lru_w_a, lru_b_a, lru_w_x, lru_b_x,
              lru_lam, w_out_a, w_out_b, w_out, b_out, g_ffn, w_up, b_up, conv_f_w, conv_f_b, w_down, b_down,
              g_final):
    y_prompt = trunk(x_prompt, g_mix, w_in, b_in, conv_a_w, conv_a_b, lru_w_a, lru_b_a, lru_w_x, lru_b_x,
                     lru_lam, w_out_a, w_out_b, w_out, b_out, g_ffn, w_up, b_up, conv_f_w, conv_f_b,
                     w_down, b_down, g_final)
    y_sample = trunk(x_sample, g_mix, w_in, b_in, conv_a_w, conv_a_b, lru_w_a, lru_b_a, lru_w_x, lru_b_x,
                     lru_lam, w_out_a, w_out_b, w_out, b_out, g_ffn, w_up, b_up, conv_f_w, conv_f_b,
                     w_down, b_down, g_final)
    return (y_prompt, y_sample)
```

```python
import functools

import jax
import jax.numpy as jnp
from jax import lax
from jax.experimental import pallas as pl
from jax.experimental.pallas import tpu as pltpu

F32 = jnp.float32
BF16 = jnp.bfloat16

D_MODEL = 2048
D_RNN = 2048
LRU_HEADS = 16
LRU_BLOCK = 128
LRU_C = 8.0
D_FOURIER = 1024
FOURIER_GROUPS = 4
FOURIER_GW = 256
D_FF = 6144
EPS = 1e-6
D_MAIN = 2 * D_RNN + 2 * D_MODEL
DFT_N2 = 128
HALO = 16
MIB = 1024 * 1024
GELU_C = 0.7978845608028654


def _sigmoid(x):
    return 0.5 * jnp.tanh(0.5 * x) + 0.5


def _gelu(x):
    return 0.5 * x * (1.0 + jnp.tanh(GELU_C * (x + 0.044715 * (x * x * x))))


def _rms_scale(x):
    return lax.rsqrt(jnp.mean(x * x, axis=-1, keepdims=True) + EPS)


def _params(sem, vmem_mib):
    return pltpu.CompilerParams(dimension_semantics=sem, vmem_limit_bytes=vmem_mib * MIB)


def _resident(shape, index_map):
    return pl.BlockSpec(shape, index_map, pipeline_mode=pl.Buffered(1))


def _inproj_kernel(x_ref, g_ref, w_ref, b_ref, pm_ref, uf_ref, n_ref, *, n_main):
    j = pl.program_id(1)

    @pl.when(j == 0)
    def _():
        x = x_ref[...]
        n_ref[...] = (x * _rms_scale(x) * g_ref[...]).astype(BF16)

    y = jnp.dot(n_ref[...], w_ref[...], preferred_element_type=F32) + b_ref[...]

    @pl.when(j < n_main)
    def _():
        pm_ref[...] = y.astype(BF16)

    @pl.when(j == n_main)
    def _():
        uf_ref[...] = y.astype(BF16)


def _in_proj(x, g_mix, w_in, b_in):
    n = x.shape[0]
    tm = min(1024, n)
    tn = D_FOURIER
    n_main = D_MAIN // tn
    return pl.pallas_call(
        functools.partial(_inproj_kernel, n_main=n_main),
        out_shape=(jax.ShapeDtypeStruct((n, D_MAIN), BF16), jax.ShapeDtypeStruct((n, D_FOURIER), BF16)),
        grid=(n // tm, n_main + 1),
        in_specs=[
            pl.BlockSpec((tm, D_MODEL), lambda i, j: (i, 0)),
            pl.BlockSpec((1, D_MODEL), lambda i, j: (0, 0)),
            pl.BlockSpec((D_MODEL, tn), lambda i, j: (0, j)),
            pl.BlockSpec((1, tn), lambda i, j: (0, j)),
        ],
        out_specs=(
            pl.BlockSpec((tm, tn), lambda i, j: (i, jnp.minimum(j, n_main - 1))),
            pl.BlockSpec((tm, tn), lambda i, j: (i, 0)),
        ),
        scratch_shapes=[pltpu.VMEM((tm, D_MODEL), BF16)],
        compiler_params=_params(("arbitrary", "arbitrary"), 48),
        name="in_proj",
    )(x, g_mix, w_in, b_in)


def _scan_block(a, b, carry, store, reverse):
    r = a.shape[0]
    a3 = a.reshape(r // 8, 8, LRU_BLOCK)
    b3 = b.reshape(r // 8, 8, LRU_BLOCK)
    row = lax.broadcasted_iota(jnp.int32, (r // 8, 8, LRU_BLOCK), 1)
    for k in (1, 2, 4):
        if reverse:
            keep = row < 8 - k
            shift = 8 - k
        else:
            keep = row >= k
            shift = k
        a_s = jnp.where(keep, pltpu.roll(a3, shift, axis=1), 1.0)
        b_s = jnp.where(keep, pltpu.roll(b3, shift, axis=1), 0.0)
        b3 = a3 * b_s + b3
        a3 = a3 * a_s
    groups = range(r // 8 - 1, -1, -1) if reverse else range(r // 8)
    edge = 0 if reverse else 7
    for g in groups:
        hg = b3[g] + a3[g] * carry
        store(g, hg)
        carry = jnp.broadcast_to(hg[edge:edge + 1, :], (8, LRU_BLOCK))
    return carry


def _lru_kernel(*refs, reverse, fused, tc, rb, nchunks):
    if fused:
        (xm_ref, xp_ref, xn_ref, cw_ref, cb_ref, wcat_ref, ba_ref, bx_ref, lam_ref,
         hf_ref, ug_ref, woa_ref, out_ref, xe_ref, hs_ref, carry_ref) = refs
    else:
        (xm_ref, xp_ref, xn_ref, cw_ref, cb_ref, wcat_ref, ba_ref, bx_ref, lam_ref,
         out_ref, xe_ref, hs_ref, carry_ref) = refs
    c = pl.program_id(1)
    cidx = nchunks - 1 - c if reverse else c

    @pl.when(c == 0)
    def _():
        carry_ref[...] = jnp.zeros_like(carry_ref)

    xe_ref[0:HALO, :] = jnp.where(cidx > 0, xp_ref[...].astype(F32), 0.0)
    xe_ref[HALO:HALO + tc, :] = xm_ref[...].astype(F32)
    xe_ref[HALO + tc:, :] = jnp.where(cidx < nchunks - 1, xn_ref[...].astype(F32), 0.0)

    def head_body(h, _):
        lo = pl.multiple_of(h * LRU_BLOCK, LRU_BLOCK)
        lanes = pl.ds(lo, LRU_BLOCK)
        w0 = cw_ref[0:1, lanes]
        w1 = cw_ref[1:2, lanes]
        w2 = cw_ref[2:3, lanes]
        w3 = cw_ref[3:4, lanes]
        cbias = cb_ref[:, lanes]
        wc = wcat_ref[h]
        bah = ba_ref[:, lanes]
        bxh = bx_ref[:, lanes]
        c8 = LRU_C * jax.nn.log_sigmoid(lam_ref[:, lanes])
        carry = carry_ref[:, lanes]
        blocks = range(tc // rb - 1, -1, -1) if reverse else range(tc // rb)
        for q in blocks:
            r0 = q * rb
            u = (w0 * xe_ref[pl.ds(HALO - 2 + r0, rb), lanes] + w1 * xe_ref[pl.ds(HALO - 1 + r0, rb), lanes]
                 + w2 * xe_ref[pl.ds(HALO + r0, rb), lanes] + w3 * xe_ref[pl.ds(HALO + 1 + r0, rb), lanes] + cbias)
            gates = jnp.dot(u.astype(BF16), wc, preferred_element_type=F32)
            r = _sigmoid(gates[:, :LRU_BLOCK] + bah)
            i = _sigmoid(gates[:, LRU_BLOCK:] + bxh)
            a = jnp.exp(r * c8)
            mult = jnp.sqrt(jnp.maximum(1.0 - a * a, 0.0))
            bterm = mult * i * u

            def store(g, hg, r0=r0):
                hs_ref[pl.ds(r0 + g * 8, 8), lanes] = hg

            carry = _scan_block(a, bterm, carry, store, reverse)
        carry_ref[:, lanes] = carry
        return 0

    lax.fori_loop(0, LRU_HEADS, head_body, 0)

    if fused:
        hr = hf_ref[...].astype(F32) + hs_ref[...]
        act = (_gelu(ug_ref[...].astype(F32)) * hr).astype(BF16)
        out_ref[...] = jnp.dot(act, woa_ref[...], preferred_element_type=F32).astype(BF16)
    else:
        out_ref[...] = hs_ref[...].astype(BF16)


def _lru(pm, bsz, s, d, cw, cb, wcat, ba, bx, lam, hf=None, woa=None):
    n = bsz * s
    tc = min(512, s)
    rb = min(256, tc)
    nchunks = s // tc
    reverse = d == 1
    fused = hf is not None

    def chunk(c):
        return nchunks - 1 - c if reverse else c

    main = lambda b, c: (b * nchunks + chunk(c), 0)
    prev = lambda b, c: (jnp.maximum((b * s + chunk(c) * tc) // HALO - 1, 0), 0)
    nxt = lambda b, c: (jnp.minimum((b * s + (chunk(c) + 1) * tc) // HALO, n // HALO - 1), 0)
    vec = lambda b, c: (0, 0)
    in_specs = [
        pl.BlockSpec((tc, D_RNN), main),
        pl.BlockSpec((HALO, D_RNN), prev),
        pl.BlockSpec((HALO, D_RNN), nxt),
        pl.BlockSpec((4, D_RNN), vec),
        pl.BlockSpec((1, D_RNN), vec),
        pl.BlockSpec((LRU_HEADS, LRU_BLOCK, 2 * LRU_BLOCK), lambda b, c: (0, 0, 0)),
        pl.BlockSpec((1, D_RNN), vec),
        pl.BlockSpec((1, D_RNN), vec),
        pl.BlockSpec((1, D_RNN), vec),
    ]
    args = [pm, pm, pm, cw, cb, wcat, ba, bx, lam]
    if fused:
        in_specs += [
            pl.BlockSpec((tc, D_RNN), main),
            pl.BlockSpec((tc, D_RNN), lambda b, c: (b * nchunks + chunk(c), 1)),
            _resident((D_RNN, D_MODEL), vec),
        ]
        args += [hf, pm, woa]
    return pl.pallas_call(
        functools.partial(_lru_kernel, reverse=reverse, fused=fused, tc=tc, rb=rb, nchunks=nchunks),
        out_shape=jax.ShapeDtypeStruct((n, D_MODEL), BF16),
        grid=(bsz, nchunks),
        in_specs=in_specs,
        out_specs=pl.BlockSpec((tc, D_MODEL), main),
        scratch_shapes=[
            pltpu.VMEM((tc + 2 * HALO, D_RNN), F32),
            pltpu.VMEM((tc, D_RNN), F32),
            pltpu.VMEM((8, D_RNN), F32),
        ],
        compiler_params=_params(("arbitrary", "arbitrary"), 48),
        name="lru_bwd_out" if fused else "lru_fwd",
    )(*args)


def _dft_mats(s):
    n1 = s // DFT_N2
    n2 = DFT_N2

    def cs(rows, cols, period):
        k = (jnp.arange(rows, dtype=jnp.int32)[:, None] * jnp.arange(cols, dtype=jnp.int32)[None, :]) % period
        ang = k.astype(F32) * (2.0 * jnp.pi / period)
        return jnp.cos(ang), jnp.sin(ang)

    c1, s1 = cs(n1, n1, n1)
    f1 = jnp.concatenate([c1, -s1], axis=0).astype(BF16)
    twc, tws = cs(n1, n2, s)
    c2, s2 = cs(n2, n2, n2)
    scale = 1.0 / jnp.sqrt(jnp.float32(s * FOURIER_GW))
    f2 = (jnp.concatenate([jnp.concatenate([c2, s2], axis=1),
                           jnp.concatenate([-s2, c2], axis=1)], axis=0) * scale).astype(BF16)
    return f1, twc, tws, f2


def _dft1_kernel(x_ref, f1_ref, twc_ref, tws_ref, yr_ref, yi_ref, *, n1, nb):
    y = jnp.dot(f1_ref[...], x_ref[...], preferred_element_type=F32)
    for m in range(nb):
        cols = slice(m * D_FOURIER, (m + 1) * D_FOURIER)
        yr = y[:n1, cols]
        yi = y[n1:, cols]
        cm = twc_ref[0, :, m:m + 1]
        sm = tws_ref[0, :, m:m + 1]
        yr_ref[:, cols] = (yr * cm + yi * sm).astype(BF16)
        yi_ref[:, cols] = (yi * cm - yr * sm).astype(BF16)


def _dft2_kernel(yr_ref, yi_ref, f2_ref, z_ref, *, kb):
    n2 = DFT_N2
    for k in range(kb):
        rows = slice(k * n2, (k + 1) * n2)
        yy = jnp.concatenate([yr_ref[rows, :], yi_ref[rows, :]], axis=0)
        z = jnp.dot(f2_ref[...], yy, preferred_element_type=F32)
        base = k * 2 * D_FOURIER
        z_ref[:, base:base + D_FOURIER] = z[:n2].astype(BF16)
        z_ref[:, base + D_FOURIER:base + 2 * D_FOURIER] = z[n2:].astype(BF16)


def _seq_dft(uf, bsz, s):
    n = bsz * s
    n2 = DFT_N2
    n1 = s // n2
    nb = 8
    kb = min(8, n1)
    f1, twc, tws, f2 = _dft_mats(s)
    tw_shape = (n2 // nb, n1, nb)
    twc = twc.reshape(n1, n2 // nb, nb).transpose(1, 0, 2)
    tws = tws.reshape(n1, n2 // nb, nb).transpose(1, 0, 2)
    wb = nb * D_FOURIER
    x2 = uf.reshape(bsz * n1, n2 * D_FOURIER)
    yr, yi = pl.pallas_call(
        functools.partial(_dft1_kernel, n1=n1, nb=nb),
        out_shape=(jax.ShapeDtypeStruct(x2.shape, BF16),) * 2,
        grid=(bsz, n2 // nb),
        in_specs=[
            pl.BlockSpec((n1, wb), lambda b, j: (b, j)),
            pl.BlockSpec((2 * n1, n1), lambda b, j: (0, 0)),
            pl.BlockSpec((1, n1, nb), lambda b, j: (j, 0, 0)),
            pl.BlockSpec((1, n1, nb), lambda b, j: (j, 0, 0)),
        ],
        out_specs=(pl.BlockSpec((n1, wb), lambda b, j: (b, j)),) * 2,
        compiler_params=_params(("arbitrary", "arbitrary"), 48),
        name="dft_stage1",
    )(x2, f1, twc, tws)
    del tw_shape
    yr = yr.reshape(n, D_FOURIER)
    yi = yi.reshape(n, D_FOURIER)
    z = pl.pallas_call(
        functools.partial(_dft2_kernel, kb=kb),
        out_shape=jax.ShapeDtypeStruct((bsz * n2, n1 * 2 * D_FOURIER), BF16),
        grid=(bsz, n1 // kb),
        in_specs=[
            pl.BlockSpec((kb * n2, D_FOURIER), lambda b, q: (b * (n1 // kb) + q, 0)),
            pl.BlockSpec((kb * n2, D_FOURIER), lambda b, q: (b * (n1 // kb) + q, 0)),
            pl.BlockSpec((2 * n2, 2 * n2), lambda b, q: (0, 0)),
        ],
        out_specs=pl.BlockSpec((n2, kb * 2 * D_FOURIER), lambda b, q: (b, q)),
        compiler_params=_params(("arbitrary", "arbitrary"), 48),
        name="dft_stage2",
    )(yr, yi, f2)
    return z.reshape(n, 2 * D_FOURIER)


def _fold_kernel(cc_ref, sc_ref, w_ref, o_ref):
    w = w_ref[...]
    o_ref[0] = jnp.dot(cc_ref[...], w, preferred_element_type=F32, precision=lax.Precision.HIGHEST).astype(BF16)
    o_ref[1] = jnp.dot(sc_ref[...], w, preferred_element_type=F32, precision=lax.Precision.HIGHEST).astype(BF16)


def _fold_channel_dft(w_out_b):
    k = (jnp.arange(FOURIER_GW, dtype=jnp.int32)[:, None] * jnp.arange(FOURIER_GW, dtype=jnp.int32)[None, :]) % FOURIER_GW
    ang = k.astype(F32) * (2.0 * jnp.pi / FOURIER_GW)
    out = pl.pallas_call(
        _fold_kernel,
        out_shape=jax.ShapeDtypeStruct((2, D_FOURIER, D_MODEL), BF16),
        grid=(FOURIER_GROUPS,),
        in_specs=[
            pl.BlockSpec((FOURIER_GW, FOURIER_GW), lambda g: (0, 0)),
            pl.BlockSpec((FOURIER_GW, FOURIER_GW), lambda g: (0, 0)),
            pl.BlockSpec((FOURIER_GW, D_MODEL), lambda g: (g, 0)),
        ],
        out_specs=pl.BlockSpec((2, FOURIER_GW, D_MODEL), lambda g: (0, g, 0)),
        compiler_params=_params(("arbitrary",), 32),
        name="fold_channel_dft",
    )(jnp.cos(ang), jnp.sin(ang), w_out_b)
    return out.reshape(2 * D_FOURIER, D_MODEL)


def _merge_kernel(x_ref, ya_ref, z_ref, za_ref, zb_ref, wb_ref, wo_ref, bo_ref, g_ref, h_ref, n_ref):
    yb = jnp.dot(z_ref[...], wb_ref[...], preferred_element_type=F32)
    m = (_sigmoid(za_ref[...].astype(F32)) * ya_ref[...].astype(F32)
         + _sigmoid(zb_ref[...].astype(F32)) * yb)
    h = x_ref[...] + jnp.dot(m.astype(BF16), wo_ref[...], preferred_element_type=F32) + bo_ref[...]
    h_ref[...] = h
    n_ref[...] = (h * _rms_scale(h) * g_ref[...]).astype(BF16)


def _merge(x, ya, z, pm, wb, w_out, b_out, g_ffn):
    n = x.shape[0]
    t = min(256, n)
    tile = lambda i: (i, 0)
    vec = lambda i: (0, 0)
    return pl.pallas_call(
        _merge_kernel,
        out_shape=(jax.ShapeDtypeStruct((n, D_MODEL), F32), jax.ShapeDtypeStruct((n, D_MODEL), BF16)),
        grid=(n // t,),
        in_specs=[
            pl.BlockSpec((t, D_MODEL), tile),
            pl.BlockSpec((t, D_MODEL), tile),
            pl.BlockSpec((t, 2 * D_FOURIER), tile),
            pl.BlockSpec((t, D_MODEL), lambda i: (i, 2)),
            pl.BlockSpec((t, D_MODEL), lambda i: (i, 3)),
            _resident((2 * D_FOURIER, D_MODEL), vec),
            _resident((D_MODEL, D_MODEL), vec),
            pl.BlockSpec((1, D_MODEL), vec),
            pl.BlockSpec((1, D_MODEL), vec),
        ],
        out_specs=(pl.BlockSpec((t, D_MODEL), tile), pl.BlockSpec((t, D_MODEL), tile)),
        compiler_params=_params(("arbitrary",), 48),
        name="merge_out_proj",
    )(x, ya, z, pm, pm, wb, w_out, b_out, g_ffn)


def _ffn_kernel(nm_ref, np_ref, nn_ref, h_ref, wg_ref, wv_ref, bg_ref, bv_ref, cwg_ref, cwv_ref, cbg_ref, cbv_ref,
                wd_ref, bd_ref, gf_ref, o_ref, ne_ref, ug_ref, uv_ref, *, t, tiles_per_seq, nj):
    i = pl.program_id(0)
    j = pl.program_id(1)
    ti = i % tiles_per_seq

    @pl.when(j == 0)
    def _():
        ne_ref[0:HALO, :] = np_ref[...]
        ne_ref[HALO:HALO + t, :] = nm_ref[...]
        ne_ref[HALO + t:, :] = nn_ref[...]
        o_ref[...] = h_ref[...] + bd_ref[...]

    lhs = ne_ref[...]
    row = lax.broadcasted_iota(jnp.int32, (t + 2 * HALO, 1), 0)
    valid = jnp.logical_and(jnp.logical_or(row >= HALO, ti > 0),
                            jnp.logical_or(row < HALO + t, ti < tiles_per_seq - 1))
    ug_ref[...] = jnp.where(valid, jnp.dot(lhs, wg_ref[...], preferred_element_type=F32) + bg_ref[...], 0.0)
    uv_ref[...] = jnp.where(valid, jnp.dot(lhs, wv_ref[...], preferred_element_type=F32) + bv_ref[...], 0.0)

    def conv(u_ref, cw_ref, cb_ref):
        return (cw_ref[0:1, :] * u_ref[pl.ds(HALO - 1, t), :] + cw_ref[1:2, :] * u_ref[pl.ds(HALO, t), :]
                + cw_ref[2:3, :] * u_ref[pl.ds(HALO + 1, t), :] + cb_ref[...])

    act = (_gelu(conv(ug_ref, cwg_ref, cbg_ref)) * conv(uv_ref, cwv_ref, cbv_ref)).astype(BF16)
    o_ref[...] += jnp.dot(act, wd_ref[...], preferred_element_type=F32)

    @pl.when(j == nj - 1)
    def _():
        h = o_ref[...]
        o_ref[...] = h * _rms_scale(h) * gf_ref[...]


def _ffn(n2, h1, bsz, s, w_up, b_up, conv_w, conv_b, w_down, b_down, g_final):
    n = bsz * s
    t = min(512, s)
    c = 512
    nj = D_FF // c
    tiles_per_seq = s // t
    tile = lambda i, j: (i, 0)
    vec = lambda i, j: (0, 0)
    gate = lambda i, j: (0, j)
    val = lambda i, j: (0, nj + j)
    return pl.pallas_call(
        functools.partial(_ffn_kernel, t=t, tiles_per_seq=tiles_per_seq, nj=nj),
        out_shape=jax.ShapeDtypeStruct((n, D_MODEL), F32),
        grid=(n // t, nj),
        in_specs=[
            pl.BlockSpec((t, D_MODEL), tile),
            pl.BlockSpec((HALO, D_MODEL), lambda i, j: (jnp.maximum(i * (t // HALO) - 1, 0), 0)),
            pl.BlockSpec((HALO, D_MODEL), lambda i, j: (jnp.minimum((i + 1) * (t // HALO), n // HALO - 1), 0)),
            pl.BlockSpec((t, D_MODEL), tile),
            pl.BlockSpec((D_MODEL, c), gate),
            pl.BlockSpec((D_MODEL, c), val),
            pl.BlockSpec((1, c), gate),
            pl.BlockSpec((1, c), val),
            pl.BlockSpec((3, c), gate),
            pl.BlockSpec((3, c), val),
            pl.BlockSpec((1, c), gate),
            pl.BlockSpec((1, c), val),
            pl.BlockSpec((c, D_MODEL), lambda i, j: (j, 0)),
            pl.BlockSpec((1, D_MODEL), vec),
            pl.BlockSpec((1, D_MODEL), vec),
        ],
        out_specs=pl.BlockSpec((t, D_MODEL), tile),
        scratch_shapes=[
            pltpu.VMEM((t + 2 * HALO, D_MODEL), BF16),
            pltpu.VMEM((t + 2 * HALO, c), F32),
            pltpu.VMEM((t + 2 * HALO, c), F32),
        ],
        compiler_params=_params(("arbitrary", "arbitrary"), 52),
        name="conv_ffn",
    )(n2, n2, n2, h1, w_up, w_up, b_up, b_up, conv_w, conv_w, conv_b, conv_b, w_down, b_down, g_final)


def _prepare(g_mix, w_in, b_in, conv_a_w, conv_a_b, lru_w_a, lru_b_a, lru_w_x, lru_b_x, lru_lam, w_out_a, w_out_b,
             w_out, b_out, g_ffn, w_up, b_up, conv_f_w, conv_f_b, w_down, b_down, g_final):
    l = 0
    fo = 2 * D_RNN
    ga = fo + D_FOURIER
    perm = lambda w: jnp.concatenate([w[..., :fo], w[..., ga:], w[..., fo:ga]], axis=-1)
    row = lambda v: v.reshape(1, -1).astype(F32)
    wcat = jnp.concatenate([lru_w_a[l], lru_w_x[l]], axis=-1).astype(BF16)
    return dict(
        g_mix=row(g_mix[l]), w_in=perm(w_in[l]).astype(BF16), b_in=row(perm(b_in[l])),
        conv_a_w=conv_a_w[l].astype(F32), conv_a_b=row(conv_a_b[l]), wcat=wcat,
        b_a=lru_b_a[l].astype(F32), b_x=lru_b_x[l].astype(F32), lam=lru_lam[l].astype(F32),
        w_out_a=w_out_a[l].astype(BF16), wb=_fold_channel_dft(w_out_b[l].astype(F32)),
        w_out=w_out[l].astype(BF16), b_out=row(b_out[l]), g_ffn=row(g_ffn[l]),
        w_up=w_up[l].astype(BF16), b_up=row(b_up[l]), conv_f_w=conv_f_w[l].astype(F32), conv_f_b=row(conv_f_b[l]),
        w_down=w_down[l].astype(BF16), b_down=row(b_down[l]), g_final=row(g_final),
    )


def _trunk(x, p):
    bsz, s, _ = x.shape
    x2 = x.reshape(bsz * s, D_MODEL).astype(F32)
    pm, uf = _in_proj(x2, p["g_mix"], p["w_in"], p["b_in"])
    lru_args = lambda d: (p["conv_a_w"], p["conv_a_b"], p["wcat"][d], p["b_a"][d:d + 1], p["b_x"][d:d + 1],
                          p["lam"][d:d + 1])
    hf = _lru(pm, bsz, s, 0, *lru_args(0))
    ya = _lru(pm, bsz, s, 1, *lru_args(1), hf=hf, woa=p["w_out_a"])
    z = _seq_dft(uf, bsz, s)
    h1, n2 = _merge(x2, ya, z, pm, p["wb"], p["w_out"], p["b_out"], p["g_ffn"])
    out = _ffn(n2, h1, bsz, s, p["w_up"], p["b_up"], p["conv_f_w"], p["conv_f_b"], p["w_down"], p["b_down"],
               p["g_final"])
    return out.reshape(bsz, s, D_MODEL).astype(x.dtype)


def kernel(x_prompt, x_sample, g_mix, w_in, b_in, conv_a_w, conv_a_b, lru_w_a, lru_b_a, lru_w_x, lru_b_x, lru_lam,
           w_out_a, w_out_b, w_out, b_out, g_ffn, w_up, b_up, conv_f_w, conv_f_b, w_down, b_down, g_final):
    p = _prepare(g_mix, w_in, b_in, conv_a_w, conv_a_b, lru_w_a, lru_b_a, lru_w_x, lru_b_x, lru_lam, w_out_a,
                 w_out_b, w_out, b_out, g_ffn, w_up, b_up, conv_f_w, conv_f_b, w_down, b_down, g_final)
    return (_trunk(x_prompt, p), _trunk(x_sample, p))
```

```python
import functools

import jax
import jax.numpy as jnp
from jax import lax
from jax.experimental import pallas as pl
from jax.experimental.pallas import tpu as pltpu

F32 = jnp.float32
BF16 = jnp.bfloat16

D_MODEL = 2048
D_RNN = 2048
LRU_HEADS = 16
LRU_BLOCK = 128
LRU_C = 8.0
D_FOURIER = 1024
FOURIER_GROUPS = 4
FOURIER_GW = 256
D_FF = 6144
EPS = 1e-6
D_MAIN = 2 * D_RNN + 2 * D_MODEL
DFT_N1 = 64
DFT_ROWS = 16
HALO = 16
MIB = 1024 * 1024
GELU_C = 0.7978845608028654
LOG2_E = 1.4426950408889634


def _sigmoid(x):
    return 0.5 * jnp.tanh(0.5 * x) + 0.5


def _gelu(x):
    return 0.5 * x * (1.0 + jnp.tanh(GELU_C * (x + 0.044715 * (x * x * x))))


def _rms_scale(x):
    return lax.rsqrt(jnp.mean(x * x, axis=-1, keepdims=True) + EPS)


def _params(sem, vmem_mib):
    return pltpu.CompilerParams(dimension_semantics=sem, vmem_limit_bytes=vmem_mib * MIB)


def _resident(shape, index_map):
    return pl.BlockSpec(shape, index_map, pipeline_mode=pl.Buffered(1))


def _inproj_kernel(x_ref, g_ref, w_ref, b_ref, pm_ref, uf_ref, n_ref, *, n_main):
    j = pl.program_id(1)

    @pl.when(j == 0)
    def _():
        x = x_ref[...]
        n_ref[...] = (x * _rms_scale(x) * g_ref[...]).astype(BF16)

    y = jnp.dot(n_ref[...], w_ref[...], preferred_element_type=F32) + b_ref[...]

    @pl.when(j < n_main)
    def _():
        pm_ref[...] = y.astype(BF16)

    @pl.when(j == n_main)
    def _():
        uf_ref[...] = y.astype(BF16)


def _in_proj(x, g_mix, w_in, b_in):
    n = x.shape[0]
    tm = min(1024, n)
    tn = D_FOURIER
    n_main = D_MAIN // tn
    return pl.pallas_call(
        functools.partial(_inproj_kernel, n_main=n_main),
        out_shape=(jax.ShapeDtypeStruct((n, D_MAIN), BF16), jax.ShapeDtypeStruct((n, D_FOURIER), BF16)),
        grid=(n // tm, n_main + 1),
        in_specs=[
            pl.BlockSpec((tm, D_MODEL), lambda i, j: (i, 0)),
            pl.BlockSpec((1, D_MODEL), lambda i, j: (0, 0)),
            pl.BlockSpec((D_MODEL, tn), lambda i, j: (0, j)),
            pl.BlockSpec((1, tn), lambda i, j: (0, j)),
        ],
        out_specs=(
            pl.BlockSpec((tm, tn), lambda i, j: (i, jnp.minimum(j, n_main - 1))),
            pl.BlockSpec((tm, tn), lambda i, j: (i, 0)),
        ),
        scratch_shapes=[pltpu.VMEM((tm, D_MODEL), BF16)],
        compiler_params=_params(("arbitrary", "arbitrary"), 48),
        name="in_proj",
    )(x, g_mix, w_in, b_in)


def _perm_rows(t, nrow):
    s, j0 = divmod(8 * t, nrow)
    return pl.ds(j0 * 8 + s, 8, stride=8)


def _fill_conv_input(k, lanes, xe_ref, xm_ref, xp_ref, xn_ref, cidx, nchunks, tc):
    nrow = tc // 8
    for g in range(tc // 16):
        v = xm_ref[pl.ds(g * 16, 16), lanes].astype(F32)
        for half in range(2):
            rows = _perm_rows(2 * g + half, nrow)
            xe_ref[k, pl.ds(16 + rows.start, 8, stride=8), :] = v[half * 8:half * 8 + 8]
    sub = lax.broadcasted_iota(jnp.int32, (8, LRU_BLOCK), 0)
    prev = jnp.where(cidx > 0, xp_ref[:, lanes].astype(F32), 0.0)
    nxt = jnp.where(cidx < nchunks - 1, xn_ref[:, lanes].astype(F32), 0.0)
    xe_ref[k, 0:8, :] = jnp.where(sub == 0, prev[HALO - 2:HALO - 1, :],
                                  pltpu.roll(xe_ref[k, tc:tc + 8, :], 1, axis=0))
    xe_ref[k, 8:16, :] = jnp.where(sub == 0, prev[HALO - 1:HALO, :],
                                   pltpu.roll(xe_ref[k, tc + 8:tc + 16, :], 1, axis=0))
    xe_ref[k, 16 + tc:, :] = jnp.where(sub == 7, nxt[0:1, :], pltpu.roll(xe_ref[k, 16:24, :], 7, axis=0))


def _head_scan(lanes, h, k, xe_ref, cw_ref, cb_ref, wcat_ref, ba_ref, bx_ref, lam_ref, hl_ref, ac_ref, carry_ref,
               *, reverse, tc, rb):
    w0 = cw_ref[0:1, lanes]
    w1 = cw_ref[1:2, lanes]
    w2 = cw_ref[2:3, lanes]
    w3 = cw_ref[3:4, lanes]
    cbias = cb_ref[:, lanes]
    wc = wcat_ref[h]
    bah = ba_ref[:, lanes]
    bxh = bx_ref[:, lanes]
    kexp = (0.5 * LRU_C * LOG2_E) * jax.nn.log_sigmoid(lam_ref[:, lanes])
    hloc = jnp.zeros((8, LRU_BLOCK), F32)
    acum = jnp.ones((8, LRU_BLOCK), F32)
    blocks = range(tc // rb - 1, -1, -1) if reverse else range(tc // rb)
    for q in blocks:
        r0 = q * rb
        u = (w0 * xe_ref[k, pl.ds(r0, rb), :] + w1 * xe_ref[k, pl.ds(r0 + 8, rb), :]
             + w2 * xe_ref[k, pl.ds(r0 + 16, rb), :] + w3 * xe_ref[k, pl.ds(r0 + 24, rb), :] + cbias)
        gates = jnp.dot(u.astype(BF16), wc, preferred_element_type=F32)
        ta = jnp.tanh(gates[:, :LRU_BLOCK] + bah)
        ti = jnp.tanh(gates[:, LRU_BLOCK:] + bxh)
        a = jnp.exp2(ta * kexp + kexp)
        y = jnp.maximum(1.0 - a * a, 0.0)
        mult = y * lax.rsqrt(jnp.maximum(y, 1e-30))
        bterm = (mult * u) * (0.5 * ti + 0.5)
        steps = range(rb // 8 - 1, -1, -1) if reverse else range(rb // 8)
        for g in steps:
            rows = slice(g * 8, g * 8 + 8)
            hloc = a[rows] * hloc + bterm[rows]
            acum = a[rows] * acum
            hl_ref[k, pl.ds(r0 + g * 8, 8), :] = hloc
            ac_ref[k, pl.ds(r0 + g * 8, 8), :] = acum
    sub = lax.broadcasted_iota(jnp.int32, (8, LRU_BLOCK), 0)
    carry = carry_ref[:, lanes]
    first, last, shift = (7, 0, 7) if reverse else (0, 7, 1)
    hin = carry
    for _ in range(7):
        hin = jnp.where(sub == first, carry, pltpu.roll(hloc + acum * hin, shift, axis=0))
    full = hloc + acum * hin
    carry_ref[:, lanes] = jnp.broadcast_to(full[last:last + 1, :], (8, LRU_BLOCK))
    return hin


def _lru_fwd_kernel(xm_ref, xp_ref, xn_ref, cw_ref, cb_ref, wcat_ref, ba_ref, bx_ref, lam_ref,
                    out_ref, xe_ref, hl_ref, ac_ref, carry_ref, *, tc, rb, nchunks):
    c = pl.program_id(1)

    @pl.when(c == 0)
    def _():
        carry_ref[...] = jnp.zeros_like(carry_ref)

    def pair_body(hp, _):
        for k in range(2):
            lanes = pl.ds(pl.multiple_of(hp * 2 * LRU_BLOCK + k * LRU_BLOCK, LRU_BLOCK), LRU_BLOCK)
            _fill_conv_input(k, lanes, xe_ref, xm_ref, xp_ref, xn_ref, c, nchunks, tc)
            hin = _head_scan(lanes, 2 * hp + k, k, xe_ref, cw_ref, cb_ref, wcat_ref, ba_ref, bx_ref, lam_ref,
                             hl_ref, ac_ref, carry_ref, reverse=False, tc=tc, rb=rb)
            hin2 = jnp.concatenate([hin, hin], axis=0)
            for g in range(tc // 16):
                rows = pl.ds(g * 16, 16)
                out_ref[rows, lanes] = (hl_ref[k, rows, :] + ac_ref[k, rows, :] * hin2).astype(BF16)
        return 0

    lax.fori_loop(0, LRU_HEADS // 2, pair_body, 0)


def _lru_bwd_kernel(xm_ref, xp_ref, xn_ref, cw_ref, cb_ref, wcat_ref, ba_ref, bx_ref, lam_ref,
                    hf_ref, ug_ref, woa_ref, out_ref, xe_ref, hl_ref, ac_ref, act_ref, actp_ref, carry_ref,
                    *, tc, rb, nchunks):
    c = pl.program_id(1)
    cidx = nchunks - 1 - jnp.minimum(c, nchunks - 1)
    nrow = tc // 8

    @pl.when(c == 0)
    def _():
        carry_ref[...] = jnp.zeros_like(carry_ref)
        actp_ref[...] = jnp.zeros_like(actp_ref)

    def pair_body(hp, _):
        lanes2 = pl.ds(pl.multiple_of(hp * 2 * LRU_BLOCK, 2 * LRU_BLOCK), 2 * LRU_BLOCK)
        for k in range(2):
            lanes = pl.ds(pl.multiple_of(hp * 2 * LRU_BLOCK + k * LRU_BLOCK, LRU_BLOCK), LRU_BLOCK)
            _fill_conv_input(k, lanes, xe_ref, xm_ref, xp_ref, xn_ref, cidx, nchunks, tc)
            hin = _head_scan(lanes, 2 * hp + k, k, xe_ref, cw_ref, cb_ref, wcat_ref, ba_ref, bx_ref, lam_ref,
                             hl_ref, ac_ref, carry_ref, reverse=True, tc=tc, rb=rb)
            hin2 = jnp.concatenate([hin, hin], axis=0)
            for g in range(tc // 16):
                rows = pl.ds(g * 16, 16)
                hl_ref[k, rows, :] = hl_ref[k, rows, :] + ac_ref[k, rows, :] * hin2 + hf_ref[rows, lanes].astype(F32)
            for g in range(tc // 16):
                rows = pl.ds(g * 16, 16)
                hr = jnp.concatenate([hl_ref[k, _perm_rows(2 * g, nrow), :],
                                      hl_ref[k, _perm_rows(2 * g + 1, nrow), :]], axis=0)
                act_ref[rows, lanes] = (_gelu(ug_ref[rows, lanes].astype(F32)) * hr).astype(BF16)
        out_ref[:, lanes2] = jnp.dot(actp_ref[...], woa_ref[:, lanes2], preferred_element_type=F32).astype(BF16)
        return 0

    lax.fori_loop(0, LRU_HEADS // 2, pair_body, 0)
    actp_ref[...] = act_ref[...]


def _lru(pm, bsz, s, d, cw, cb, wcat, ba, bx, lam, hf=None, woa=None):
    n = bsz * s
    tc = min(512, s)
    rb = min(128, tc)
    nchunks = s // tc
    fused = hf is not None

    if fused:
        scan_chunk = lambda c: nchunks - 1 - jnp.minimum(c, nchunks - 1)
        out_chunk = lambda c: nchunks - 1 - jnp.maximum(c - 1, 0)
    else:
        scan_chunk = out_chunk = lambda c: c

    main = lambda b, c: (b * nchunks + scan_chunk(c), 0)
    prev = lambda b, c: (jnp.maximum((b * s + scan_chunk(c) * tc) // HALO - 1, 0), 0)
    nxt = lambda b, c: (jnp.minimum((b * s + (scan_chunk(c) + 1) * tc) // HALO, n // HALO - 1), 0)
    vec = lambda b, c: (0, 0)
    in_specs = [
        pl.BlockSpec((tc, D_RNN), main),
        pl.BlockSpec((HALO, D_RNN), prev),
        pl.BlockSpec((HALO, D_RNN), nxt),
        pl.BlockSpec((4, D_RNN), vec),
        pl.BlockSpec((1, D_RNN), vec),
        pl.BlockSpec((LRU_HEADS, LRU_BLOCK, 2 * LRU_BLOCK), lambda b, c: (0, 0, 0)),
        pl.BlockSpec((1, D_RNN), vec),
        pl.BlockSpec((1, D_RNN), vec),
        pl.BlockSpec((1, D_RNN), vec),
    ]
    args = [pm, pm, pm, cw, cb, wcat, ba, bx, lam]
    scratch = [
        pltpu.VMEM((2, tc + 24, LRU_BLOCK), F32),
        pltpu.VMEM((2, tc, LRU_BLOCK), F32),
        pltpu.VMEM((2, tc, LRU_BLOCK), F32),
    ]
    if fused:
        in_specs += [
            pl.BlockSpec((tc, D_RNN), main),
            pl.BlockSpec((tc, D_RNN), lambda b, c: (b * nchunks + scan_chunk(c), 1)),
            _resident((D_RNN, D_MODEL), vec),
        ]
        args += [hf, pm, woa]
        scratch += [pltpu.VMEM((tc, D_RNN), BF16), pltpu.VMEM((tc, D_RNN), BF16)]
        body = functools.partial(_lru_bwd_kernel, tc=tc, rb=rb, nchunks=nchunks)
    else:
        body = functools.partial(_lru_fwd_kernel, tc=tc, rb=rb, nchunks=nchunks)
    scratch.append(pltpu.VMEM((8, D_RNN), F32))
    return pl.pallas_call(
        body,
        out_shape=jax.ShapeDtypeStruct((n, D_MODEL), BF16),
        grid=(bsz, nchunks + 1 if fused else nchunks),
        in_specs=in_specs,
        out_specs=pl.BlockSpec((tc, D_MODEL), lambda b, c: (b * nchunks + out_chunk(c), 0)),
        scratch_shapes=scratch,
        compiler_params=_params(("arbitrary", "arbitrary"), 48),
        name="lru_bwd_out" if fused else "lru_fwd",
    )(*args)


def _cos_sin(rows, cols, period):
    k = (jnp.arange(rows, dtype=jnp.int32)[:, None] * jnp.arange(cols, dtype=jnp.int32)[None, :]) % period
    ang = k.astype(F32) * (2.0 * jnp.pi / period)
    return jnp.cos(ang), jnp.sin(ang)


def _dft1_kernel(x_ref, kf_ref, twc_ref, tws_ref, yr_ref, yi_ref, *, n1):
    r = n1 * DFT_ROWS
    y = jnp.dot(kf_ref[...], x_ref[0].reshape(r, D_FOURIER), preferred_element_type=F32)
    yr = y[:r]
    yi = y[r:]
    c = twc_ref[0]
    s = tws_ref[0]
    yr_ref[0] = (yr * c + yi * s).astype(BF16).reshape(n1, DFT_ROWS, D_FOURIER)
    yi_ref[0] = (yi * c - yr * s).astype(BF16).reshape(n1, DFT_ROWS, D_FOURIER)


def _dft2_kernel(yr_ref, yi_ref, f2_ref, z_ref, *, kb, n2):
    for k in range(kb):
        rows = slice(k * n2, (k + 1) * n2)
        yy = jnp.concatenate([yr_ref[rows, :], yi_ref[rows, :]], axis=0)
        z = jnp.dot(f2_ref[...], yy, preferred_element_type=F32)
        base = k * 2 * D_FOURIER
        z_ref[:, base:base + D_FOURIER] = z[:n2].astype(BF16)
        z_ref[:, base + D_FOURIER:base + 2 * D_FOURIER] = z[n2:].astype(BF16)


def _seq_dft(uf, bsz, s):
    n = bsz * s
    n1 = DFT_N1
    n2 = s // n1
    nb = DFT_ROWS
    kb = max(1, 1024 // n2)
    c1, s1 = _cos_sin(n1, n1, n1)
    eye = jnp.eye(nb, dtype=F32)
    kf = jnp.concatenate([jnp.kron(c1, eye), jnp.kron(-s1, eye)], axis=0).astype(BF16)
    twc, tws = _cos_sin(n1, n2, s)
    tw_blocks = lambda t: t.reshape(n1, n2 // nb, nb).transpose(1, 0, 2).reshape(n2 // nb, n1 * nb, 1)
    c2, s2 = _cos_sin(n2, n2, n2)
    scale = 1.0 / jnp.sqrt(jnp.float32(s * FOURIER_GW))
    f2 = (jnp.concatenate([jnp.concatenate([c2, s2], axis=1),
                           jnp.concatenate([-s2, c2], axis=1)], axis=0) * scale).astype(BF16)
    blk = pl.BlockSpec((1, n1, nb, D_FOURIER), lambda b, j: (b, 0, j, 0))
    tw_spec = pl.BlockSpec((1, n1 * nb, 1), lambda b, j: (j, 0, 0))
    yr, yi = pl.pallas_call(
        functools.partial(_dft1_kernel, n1=n1),
        out_shape=(jax.ShapeDtypeStruct((bsz, n1, n2, D_FOURIER), BF16),) * 2,
        grid=(bsz, n2 // nb),
        in_specs=[blk, _resident((2 * n1 * nb, n1 * nb), lambda b, j: (0, 0)), tw_spec, tw_spec],
        out_specs=(blk, blk),
        compiler_params=_params(("arbitrary", "arbitrary"), 40),
        name="dft_stage1",
    )(uf.reshape(bsz, n1, n2, D_FOURIER), kf, tw_blocks(twc), tw_blocks(tws))
    yr = yr.reshape(n, D_FOURIER)
    yi = yi.reshape(n, D_FOURIER)
    return pl.pallas_call(
        functools.partial(_dft2_kernel, kb=kb, n2=n2),
        out_shape=jax.ShapeDtypeStruct((bsz * n2, n1 * 2 * D_FOURIER), BF16),
        grid=(bsz, n1 // kb),
        in_specs=[
            pl.BlockSpec((kb * n2, D_FOURIER), lambda b, q: (b * (n1 // kb) + q, 0)),
            pl.BlockSpec((kb * n2, D_FOURIER), lambda b, q: (b * (n1 // kb) + q, 0)),
            pl.BlockSpec((2 * n2, 2 * n2), lambda b, q: (0, 0)),
        ],
        out_specs=pl.BlockSpec((n2, kb * 2 * D_FOURIER), lambda b, q: (b, q)),
        compiler_params=_params(("arbitrary", "arbitrary"), 48),
        name="dft_stage2",
    )(yr, yi, f2)


def _fold_kernel(cc_ref, sc_ref, w_ref, o_ref):
    w = w_ref[...]
    o_ref[0] = jnp.dot(cc_ref[...], w, preferred_element_type=F32, precision=lax.Precision.HIGHEST).astype(BF16)
    o_ref[1] = jnp.dot(sc_ref[...], w, preferred_element_type=F32, precision=lax.Precision.HIGHEST).astype(BF16)


def _fold_channel_dft(w_out_b):
    cc, sc = _cos_sin(FOURIER_GW, FOURIER_GW, FOURIER_GW)
    out = pl.pallas_call(
        _fold_kernel,
        out_shape=jax.ShapeDtypeStruct((2, D_FOURIER, D_MODEL), BF16),
        grid=(FOURIER_GROUPS,),
        in_specs=[
            pl.BlockSpec((FOURIER_GW, FOURIER_GW), lambda g: (0, 0)),
            pl.BlockSpec((FOURIER_GW, FOURIER_GW), lambda g: (0, 0)),
            pl.BlockSpec((FOURIER_GW, D_MODEL), lambda g: (g, 0)),
        ],
        out_specs=pl.BlockSpec((2, FOURIER_GW, D_MODEL), lambda g: (0, g, 0)),
        compiler_params=_params(("arbitrary",), 32),
        name="fold_channel_dft",
    )(cc, sc, w_out_b)
    return out.reshape(2 * D_FOURIER, D_MODEL)


def _merge_kernel(x_ref, ya_ref, z_ref, za_ref, zb_ref, p_ref, wb_ref, wo_ref, bo_ref, g_ref, h_ref, n_ref):
    t = DFT_ROWS * DFT_ROWS
    w2 = 2 * D_FOURIER
    zk = jnp.concatenate([z_ref[:, k * w2:(k + 1) * w2] for k in range(DFT_ROWS)], axis=0)
    zt = jnp.dot(p_ref[...], zk, preferred_element_type=F32).astype(BF16)
    yb = jnp.dot(zt, wb_ref[...], preferred_element_type=F32)
    za = za_ref[0].reshape(t, D_MODEL).astype(F32)
    zb = zb_ref[0].reshape(t, D_MODEL).astype(F32)
    ya = ya_ref[0].reshape(t, D_MODEL).astype(F32)
    m = _sigmoid(za) * ya + _sigmoid(zb) * yb
    h = (x_ref[0].reshape(t, D_MODEL) + jnp.dot(m.astype(BF16), wo_ref[...], preferred_element_type=F32)
         + bo_ref[...])
    h_ref[0] = h.reshape(DFT_ROWS, DFT_ROWS, D_MODEL)
    n_ref[0] = (h * _rms_scale(h) * g_ref[...]).astype(BF16).reshape(DFT_ROWS, DFT_ROWS, D_MODEL)


def _merge(x, ya, z, pm, bsz, s, wb, w_out, b_out, g_ffn):
    n = bsz * s
    n1 = DFT_N1
    n2 = s // n1
    r = DFT_ROWS
    idx = jnp.arange(r * r, dtype=jnp.int32)
    perm = ((idx % r) * r + idx // r)[:, None] == idx[None, :]
    tok = lambda col: pl.BlockSpec((1, r, r, D_MODEL), lambda b, q2, q1: (b, q2, q1, col))
    vec = lambda b, q2, q1: (0, 0)
    view = lambda a: a.reshape(bsz, n2, n1, a.shape[-1])
    h1, nrm = pl.pallas_call(
        _merge_kernel,
        out_shape=(jax.ShapeDtypeStruct((bsz, n2, n1, D_MODEL), F32),
                   jax.ShapeDtypeStruct((bsz, n2, n1, D_MODEL), BF16)),
        grid=(bsz, n2 // r, n1 // r),
        in_specs=[
            tok(0),
            tok(0),
            pl.BlockSpec((r, r * 2 * D_FOURIER), lambda b, q2, q1: (b * (n2 // r) + q2, q1)),
            tok(2),
            tok(3),
            pl.BlockSpec((r * r, r * r), vec),
            _resident((2 * D_FOURIER, D_MODEL), vec),
            _resident((D_MODEL, D_MODEL), vec),
            pl.BlockSpec((1, D_MODEL), vec),
            pl.BlockSpec((1, D_MODEL), vec),
        ],
        out_specs=(tok(0), tok(0)),
        compiler_params=_params(("arbitrary", "arbitrary", "arbitrary"), 48),
        name="merge_out_proj",
    )(view(x), view(ya), z, view(pm), view(pm), perm.astype(BF16), wb, w_out, b_out, g_ffn)
    return h1.reshape(n, D_MODEL), nrm.reshape(n, D_MODEL)


def _ffn_kernel(nm_ref, np_ref, nn_ref, h_ref, wg_ref, wv_ref, bg_ref, bv_ref, cwg_ref, cwv_ref, cbg_ref, cbv_ref,
                wd_ref, bd_ref, gf_ref, o_ref, ne_ref, ug_ref, uv_ref, *, t, tiles_per_seq, nj):
    i = pl.program_id(0)
    j = pl.program_id(1)
    ti = i % tiles_per_seq

    @pl.when(j == 0)
    def _():
        ne_ref[0:HALO, :] = np_ref[...]
        ne_ref[HALO:HALO + t, :] = nm_ref[...]
        ne_ref[HALO + t:, :] = nn_ref[...]
        o_ref[...] = h_ref[...] + bd_ref[...]

    lhs = ne_ref[...]
    row = lax.broadcasted_iota(jnp.int32, (t + 2 * HALO, 1), 0)
    valid = jnp.logical_and(jnp.logical_or(row >= HALO, ti > 0),
                            jnp.logical_or(row < HALO + t, ti < tiles_per_seq - 1))
    ug_ref[...] = jnp.where(valid, jnp.dot(lhs, wg_ref[...], preferred_element_type=F32) + bg_ref[...], 0.0)
    uv_ref[...] = jnp.where(valid, jnp.dot(lhs, wv_ref[...], preferred_element_type=F32) + bv_ref[...], 0.0)

    def conv(u_ref, cw_ref, cb_ref):
        return (cw_ref[0:1, :] * u_ref[pl.ds(HALO - 1, t), :] + cw_ref[1:2, :] * u_ref[pl.ds(HALO, t), :]
                + cw_ref[2:3, :] * u_ref[pl.ds(HALO + 1, t), :] + cb_ref[...])

    act = (_gelu(conv(ug_ref, cwg_ref, cbg_ref)) * conv(uv_ref, cwv_ref, cbv_ref)).astype(BF16)
    o_ref[...] += jnp.dot(act, wd_ref[...], preferred_element_type=F32)

    @pl.when(j == nj - 1)
    def _():
        h = o_ref[...]
        o_ref[...] = h * _rms_scale(h) * gf_ref[...]


def _ffn(n2, h1, bsz, s, w_up, b_up, conv_w, conv_b, w_down, b_down, g_final):
    n = bsz * s
    t = min(512, s)
    c = 1024
    nj = D_FF // c
    tiles_per_seq = s // t
    tile = lambda i, j: (i, 0)
    vec = lambda i, j: (0, 0)
    gate = lambda i, j: (0, j)
    val = lambda i, j: (0, nj + j)
    return pl.pallas_call(
        functools.partial(_ffn_kernel, t=t, tiles_per_seq=tiles_per_seq, nj=nj),
        out_shape=jax.ShapeDtypeStruct((n, D_MODEL), F32),
        grid=(n // t, nj),
        in_specs=[
            pl.BlockSpec((t, D_MODEL), tile),
            pl.BlockSpec((HALO, D_MODEL), lambda i, j: (jnp.maximum(i * (t // HALO) - 1, 0), 0)),
            pl.BlockSpec((HALO, D_MODEL), lambda i, j: (jnp.minimum((i + 1) * (t // HALO), n // HALO - 1), 0)),
            pl.BlockSpec((t, D_MODEL), tile),
            pl.BlockSpec((D_MODEL, c), gate),
            pl.BlockSpec((D_MODEL, c), val),
            pl.BlockSpec((1, c), gate),
            pl.BlockSpec((1, c), val),
            pl.BlockSpec((3, c), gate),
            pl.BlockSpec((3, c), val),
            pl.BlockSpec((1, c), gate),
            pl.BlockSpec((1, c), val),
            pl.BlockSpec((c, D_MODEL), lambda i, j: (j, 0)),
            pl.BlockSpec((1, D_MODEL), vec),
            pl.BlockSpec((1, D_MODEL), vec),
        ],
        out_specs=pl.BlockSpec((t, D_MODEL), tile),
        scratch_shapes=[
            pltpu.VMEM((t + 2 * HALO, D_MODEL), BF16),
            pltpu.VMEM((t + 2 * HALO, c), F32),
            pltpu.VMEM((t + 2 * HALO, c), F32),
        ],
        compiler_params=_params(("arbitrary", "arbitrary"), 57),
        name="conv_ffn",
    )(n2, n2, n2, h1, w_up, w_up, b_up, b_up, conv_w, conv_w, conv_b, conv_b, w_down, b_down, g_final)


def _prepare(g_mix, w_in, b_in, conv_a_w, conv_a_b, lru_w_a, lru_b_a, lru_w_x, lru_b_x, lru_lam, w_out_a, w_out_b,
             w_out, b_out, g_ffn, w_up, b_up, conv_f_w, conv_f_b, w_down, b_down, g_final):
    l = 0
    fo = 2 * D_RNN
    ga = fo + D_FOURIER
    perm = lambda w: jnp.concatenate([w[..., :fo], w[..., ga:], w[..., fo:ga]], axis=-1)
    row = lambda v: v.reshape(1, -1).astype(F32)
    wcat = (0.5 * jnp.concatenate([lru_w_a[l], lru_w_x[l]], axis=-1)).astype(BF16)
    return dict(
        g_mix=row(g_mix[l]), w_in=perm(w_in[l]).astype(BF16), b_in=row(perm(b_in[l])),
        conv_a_w=conv_a_w[l].astype(F32), conv_a_b=row(conv_a_b[l]), wcat=wcat,
        b_a=0.5 * lru_b_a[l].astype(F32), b_x=0.5 * lru_b_x[l].astype(F32), lam=lru_lam[l].astype(F32),
        w_out_a=w_out_a[l].astype(BF16), wb=_fold_channel_dft(w_out_b[l].astype(F32)),
        w_out=w_out[l].astype(BF16), b_out=row(b_out[l]), g_ffn=row(g_ffn[l]),
        w_up=w_up[l].astype(BF16), b_up=row(b_up[l]), conv_f_w=conv_f_w[l].astype(F32), conv_f_b=row(conv_f_b[l]),
        w_down=w_down[l].astype(BF16), b_down=row(b_down[l]), g_final=row(g_final),
    )


def _trunk(x, p):
    bsz, s, _ = x.shape
    x2 = x.reshape(bsz * s, D_MODEL).astype(F32)
    pm, uf = _in_proj(x2, p["g_mix"], p["w_in"], p["b_in"])
    lru_args = lambda d: (p["conv_a_w"], p["conv_a_b"], p["wcat"][d], p["b_a"][d:d + 1], p["b_x"][d:d + 1],
                          p["lam"][d:d + 1])
    hf = _lru(pm, bsz, s, 0, *lru_args(0))
    ya = _lru(pm, bsz, s, 1, *lru_args(1), hf=hf, woa=p["w_out_a"])
    z = _seq_dft(uf, bsz, s)
    h1, n2 = _merge(x2, ya, z, pm, bsz, s, p["wb"], p["w_out"], p["b_out"], p["g_ffn"])
    out = _ffn(n2, h1, bsz, s, p["w_up"], p["b_up"], p["conv_f_w"], p["conv_f_b"], p["w_down"], p["b_down"],
               p["g_final"])
    return out.reshape(bsz, s, D_MODEL).astype(x.dtype)


def kernel(x_prompt, x_sample, g_mix, w_in, b_in, conv_a_w, conv_a_b, lru_w_a, lru_b_a, lru_w_x, lru_b_x, lru_lam,
           w_out_a, w_out_b, w_out, b_out, g_ffn, w_up, b_up, conv_f_w, conv_f_b, w_down, b_down, g_final):
    p = _prepare(g_mix, w_in, b_in, conv_a_w, conv_a_b, lru_w_a, lru_b_a, lru_w_x, lru_b_x, lru_lam, w_out_a,
                 w_out_b, w_out, b_out, g_ffn, w_up, b_up, conv_f_w, conv_f_b, w_down, b_down, g_final)
    return (_trunk(x_prompt, p), _trunk(x_sample, p))
```
